```python
import math
import jax, jax.numpy as jnp
from jax import lax
import numpy as np

D_MODEL = 1024
BATCH = 8
SEQ = 2048
DEPTH = 2
DEC_BATCH = 128
DEC_SEQ = 1
PAST_LEN = 16384
PAGE_SIZE = 128

ROPE_THETA = 10000.0
NORM_EPS = 1e-6
Q_BLOCK = 128
D_FF = 4 * D_MODEL

MLA_HEADS = 8
MLA_NOPE = 64
MLA_ROPE = 32
MLA_V = 64
MLA_Q_RANK = 256
MLA_KV_RANK = 256

DIFF_HEADS = 4
DIFF_KV_HEADS = 2
DIFF_GROUP = DIFF_HEADS // DIFF_KV_HEADS
DIFF_HD = 64
DIFF_VD = 2 * DIFF_HD
DIFF_LAYER_INDEX = 0
DIFF_LAMBDA_INIT = 0.8 - 0.6 * math.exp(-0.3 * DIFF_LAYER_INDEX)

DSA_HEADS = 16
DSA_KV_HEADS = 2
DSA_GROUP = DSA_HEADS // DSA_KV_HEADS
DSA_HD = 64
IDX_HEADS = 8
IDX_DIM = 64
IDX_TOPK_MAX = 256
IDX_WEIGHT_SCALE = (IDX_HEADS ** -0.5) * (IDX_DIM ** -0.5)

L0_WIDTHS = (MLA_Q_RANK, MLA_KV_RANK + MLA_ROPE, DIFF_HEADS * 2 * DIFF_HD,
             DIFF_KV_HEADS * 2 * DIFF_HD, DIFF_KV_HEADS * DIFF_VD)
L0_IN = sum(L0_WIDTHS)
L0_MIX = MLA_HEADS * MLA_V + DIFF_HEADS * DIFF_VD
L1_WIDTHS = (DSA_HEADS * DSA_HD, DSA_KV_HEADS * DSA_HD, DSA_KV_HEADS * DSA_HD,
             IDX_HEADS * IDX_DIM, IDX_HEADS, IDX_DIM)
L1_IN = sum(L1_WIDTHS)
L1_MIX = DSA_HEADS * DSA_HD

kernel_name = 'mla_diff_dsa_adaln_hybrid_step'


def split_cols(z, widths):
    out, o = [], 0
    for w in widths:
        out.append(z[..., o:o + w])
        o += w
    return out


def rmsnorm(x, g):
    x32 = x.astype(jnp.float32)
    y = x32 * lax.rsqrt(jnp.mean(x32 * x32, axis=-1, keepdims=True) + NORM_EPS)
    return (y * g.astype(jnp.float32)).astype(x.dtype)


def rope(x, pos):
    d = x.shape[-1]
    inv = jnp.power(ROPE_THETA, -jnp.arange(0, d, 2, dtype=jnp.float32) / d)
    ang = pos[:, None] * inv[None, :]
    ang = ang.reshape((1, ang.shape[0]) + (1,) * (x.ndim - 3) + (d // 2,))
    cos, sin = jnp.cos(ang), jnp.sin(ang)
    x32 = x.astype(jnp.float32)
    x1, x2 = x32[..., :d // 2], x32[..., d // 2:]
    return jnp.concatenate([x1 * cos - x2 * sin, x2 * cos + x1 * sin], axis=-1).astype(x.dtype)


def adaln(c, w, b):
    m = jax.nn.silu(c) @ w + b
    return jnp.split(m[:, None, :], 6, axis=-1)


def modulate(h, shift, scale):
    return h * (1.0 + scale) + shift


def dense_causal_attention(q_parts, k_parts, v, scale):
    B, S = v.shape[:2]
    nb = S // Q_BLOCK
    kpos = jnp.arange(S)

    def blk(args):
        i, qs = args[0], args[1:]
        s = scale * sum(jnp.einsum('bqhgd,bkhd->bhgqk', q, k, preferred_element_type=jnp.float32)
                        for q, k in zip(qs, k_parts))
        qpos = i * Q_BLOCK + jnp.arange(Q_BLOCK)
        s = jnp.where(kpos[None, :] <= qpos[:, None], s, -jnp.inf)
        p = jax.nn.softmax(s, axis=-1).astype(v.dtype)
        return jnp.einsum('bhgqk,bkhd->bqhgd', p, v)

    qb = [q.reshape((B, nb, Q_BLOCK) + q.shape[2:]).swapaxes(0, 1) for q in q_parts]
    out = lax.map(blk, (jnp.arange(nb), *qb))
    return out.swapaxes(0, 1).reshape((B, S) + out.shape[3:])


def paged_causal_attention(q_parts, k_page_fns, v_page_fn, page_table, k_new_parts, v_new, scale):
    B, T, Hk, G = q_parts[0].shape[:4]
    Dv = v_new.shape[-1]

    def scores(k_parts):
        return scale * sum(jnp.einsum('bthgd,bshd->bhgts', q, k, preferred_element_type=jnp.float32)
                           for q, k in zip(q_parts, k_parts))

    def merge(carry, s, v):
        m, l, acc = carry
        m_new = jnp.maximum(m, s.max(-1))
        corr = jnp.exp(m - m_new)
        p = jnp.exp(s - m_new[..., None])
        acc = acc * corr[..., None] + jnp.einsum('bhgts,bshd->bhgtd', p, v.astype(jnp.float32))
        return (m_new, l * corr + p.sum(-1), acc)

    def step(carry, phys):
        return merge(carry, scores([f(phys) for f in k_page_fns]), v_page_fn(phys)), None

    init = (jnp.full((B, Hk, G, T), -jnp.inf, jnp.float32),
            jnp.zeros((B, Hk, G, T), jnp.float32),
            jnp.zeros((B, Hk, G, T, Dv), jnp.float32))
    carry, _ = lax.scan(step, init, page_table.T)
    causal = jnp.arange(T)[None, :] <= jnp.arange(T)[:, None]
    s_new = jnp.where(causal, scores(k_new_parts), -jnp.inf)
    m, l, acc = merge(carry, s_new, v_new)
    return (acc / l[..., None]).transpose(0, 3, 1, 2, 4).astype(v_new.dtype)


def index_scores(q_idx, w_idx, k_idx):
    r = jax.nn.relu(jnp.einsum('bthd,bsd->bths', q_idx, k_idx, preferred_element_type=jnp.float32))
    return jnp.einsum('bth,bths->bts', w_idx.astype(jnp.float32), r)


def sparse_attention(q, k_sel, v_sel, valid, scale):
    s = scale * jnp.einsum('bthgd,btkhd->bthgk', q, k_sel, preferred_element_type=jnp.float32)
    s = jnp.where(valid[:, :, None, None, :], s, -jnp.inf)
    p = jax.nn.softmax(s, axis=-1).astype(v_sel.dtype)
    return jnp.einsum('bthgk,btkhd->bthgd', p, v_sel)


def dsa_prompt(q, k, v, q_idx, w_idx, k_idx):
    B, S = q.shape[:2]
    topk = min(IDX_TOPK_MAX, S // 4)
    nb = S // Q_BLOCK
    kpos = jnp.arange(S)
    gather = jax.vmap(lambda a, i: a[i])

    def blk(args):
        i, qb, qib, wb = args
        qpos = i * Q_BLOCK + jnp.arange(Q_BLOCK)
        sc = jnp.where(kpos[None, :] <= qpos[:, None], index_scores(qib, wb, k_idx), -jnp.inf)
        _, sel = lax.top_k(sc, topk)
        valid = sel <= qpos[None, :, None]
        return sparse_attention(qb, gather(k, sel), gather(v, sel), valid, DSA_HD ** -0.5)

    blocks = lambda a: a.reshape((B, nb, Q_BLOCK) + a.shape[2:]).swapaxes(0, 1)
    out = lax.map(blk, (jnp.arange(nb), blocks(q), blocks(q_idx), blocks(w_idx)))
    return out.swapaxes(0, 1).reshape((B, S) + out.shape[3:])


def dsa_sample(q, k_new, v_new, q_idx, w_idx, kidx_new, cache_k, cache_v, cache_kidx, page_table):
    B, T = q.shape[:2]
    past = page_table.shape[1] * PAGE_SIZE
    topk = min(IDX_TOPK_MAX, (past + T) // 4)
    past_sc = lax.map(lambda phys: index_scores(q_idx, w_idx, cache_kidx[phys]), page_table.T)
    past_sc = past_sc.transpose(1, 2, 0, 3).reshape(B, T, past)
    causal = jnp.arange(T)[None, :] <= jnp.arange(T)[:, None]
    new_sc = jnp.where(causal, index_scores(q_idx, w_idx, kidx_new), -jnp.inf)
    _, sel = lax.top_k(jnp.concatenate([past_sc, new_sc], axis=-1), topk)
    is_past = sel < past
    sp = jnp.clip(sel, 0, max(past - 1, 0))
    phys = (jnp.take_along_axis(page_table, (sp // PAGE_SIZE).reshape(B, -1), axis=1).reshape(sp.shape)
            * PAGE_SIZE + sp % PAGE_SIZE)
    sn = jnp.clip(sel - past, 0, T - 1)
    gather = jax.vmap(lambda a, i: a[i])

    def pick(pool, new):
        flat = pool.reshape((-1,) + pool.shape[2:])
        return jnp.where(is_past[..., None, None], flat[phys], gather(new, sn))

    valid = sel <= past + jnp.arange(T)[None, :, None]
    return sparse_attention(q, pick(cache_k, k_new), pick(cache_v, v_new), valid, DSA_HD ** -0.5)


def mixer_ab(h, pos, w_in_ab, g_mla_q, w_mla_q_up, g_mla_kv, w_mla_uk, w_mla_uv,
             diff_lambda_q1, diff_lambda_k1, diff_lambda_q2, diff_lambda_k2, g_diff_sub, w_out_ab,
             paged=None):
    B, S, _ = h.shape
    q_a, kv_a, dq, dk, dv = split_cols(h @ w_in_ab, L0_WIDTHS)
    q = (rmsnorm(q_a, g_mla_q) @ w_mla_q_up).reshape(B, S, MLA_HEADS, MLA_NOPE + MLA_ROPE)
    q_lat = jnp.einsum('bshd,chd->bshc', q[..., :MLA_NOPE], w_mla_uk)[:, :, None]
    q_rope = rope(q[..., MLA_NOPE:], pos)[:, :, None]
    c_kv = rmsnorm(kv_a[..., :MLA_KV_RANK], g_mla_kv)
    k_rope = rope(kv_a[..., MLA_KV_RANK:], pos)
    dq = rope(dq.reshape(B, S, DIFF_KV_HEADS, DIFF_GROUP, 2, DIFF_HD), pos)
    dk = rope(dk.reshape(B, S, DIFF_KV_HEADS, 2, DIFF_HD), pos)
    dv = dv.reshape(B, S, DIFF_KV_HEADS, DIFF_VD)
    mla_scale = (MLA_NOPE + MLA_ROPE) ** -0.5
    diff_scale = DIFF_HD ** -0.5
    if paged is None:
        lat = dense_causal_attention([q_lat, q_rope], [c_kv[:, :, None], k_rope[:, :, None]],
                                     c_kv[:, :, None], mla_scale)
        a1 = dense_causal_attention([dq[..., 0, :]], [dk[..., 0, :]], dv, diff_scale)
        a2 = dense_causal_attention([dq[..., 1, :]], [dk[..., 1, :]], dv, diff_scale)
    else:
        lat_pool, kr_pool, dk_pool, dv_pool, page_table = paged
        lat_page = lambda ph: lat_pool[ph][:, :, None]
        kr_page = lambda ph: kr_pool[ph][:, :, None]
        lat = paged_causal_attention([q_lat, q_rope], [lat_page, kr_page], lat_page, page_table,
                                     [c_kv[:, :, None], k_rope[:, :, None]], c_kv[:, :, None], mla_scale)
        v_page = lambda ph: dv_pool[ph]
        a1 = paged_causal_attention([dq[..., 0, :]], [lambda ph: dk_pool[ph][..., 0, :]], v_page, page_table,
                                    [dk[..., 0, :]], dv, diff_scale)
        a2 = paged_causal_attention([dq[..., 1, :]], [lambda ph: dk_pool[ph][..., 1, :]], v_page, page_table,
                                    [dk[..., 1, :]], dv, diff_scale)
    mla_out = jnp.einsum('bshc,chd->bshd', lat[:, :, 0], w_mla_uv).reshape(B, S, MLA_HEADS * MLA_V)
    lam = (jnp.exp(jnp.sum(diff_lambda_q1 * diff_lambda_k1, dtype=jnp.float32))
           - jnp.exp(jnp.sum(diff_lambda_q2 * diff_lambda_k2, dtype=jnp.float32))
           + DIFF_LAMBDA_INIT).astype(a1.dtype)
    o = (a1 - lam * a2).reshape(B, S, DIFF_HEADS, DIFF_VD)
    diff_out = (rmsnorm(o, g_diff_sub) * (1.0 - DIFF_LAMBDA_INIT)).reshape(B, S, DIFF_HEADS * DIFF_VD)
    out = jnp.concatenate([mla_out, diff_out], axis=-1) @ w_out_ab
    return out, (c_kv, k_rope, dk, dv)


def mixer_c(h, pos, w_in_c, w_out_c, paged=None):
    B, S, _ = h.shape
    q, k, v, iq, iw, ik = split_cols(h @ w_in_c, L1_WIDTHS)
    q = rope(q.reshape(B, S, DSA_KV_HEADS, DSA_GROUP, DSA_HD), pos)
    k = rope(k.reshape(B, S, DSA_KV_HEADS, DSA_HD), pos)
    v = v.reshape(B, S, DSA_KV_HEADS, DSA_HD)
    q_idx = rope(iq.reshape(B, S, IDX_HEADS, IDX_DIM), pos)
    w_idx = iw * IDX_WEIGHT_SCALE
    k_idx = rope(ik, pos)
    if paged is None:
        o = dsa_prompt(q, k, v, q_idx, w_idx, k_idx)
    else:
        o = dsa_sample(q, k, v, q_idx, w_idx, k_idx, *paged)
    return o.reshape(B, S, L1_MIX) @ w_out_c, (k, v, k_idx)


def conditioned_block(x, c, mixer, w_ada_l, b_ada_l, g_mix_l, g_ff_l, w_up_l, w_down_l):
    shift_m, scale_m, gate_m, shift_f, scale_f, gate_f = adaln(c, w_ada_l, b_ada_l)
    mix, rows = mixer(modulate(rmsnorm(x, g_mix_l), shift_m, scale_m))
    x = x + gate_m * mix
    h = modulate(rmsnorm(x, g_ff_l), shift_f, scale_f)
    x = x + gate_f * (jnp.square(jax.nn.relu(h @ w_up_l)) @ w_down_l)
    return x, rows


def setup_inputs(seed: int = 0) -> dict:
    key = jax.random.key(seed)
    ks = iter(jax.random.split(key, 48))

    def nrm(shape, scale=1.0):
        return jax.random.normal(next(ks), shape, jnp.float32) * scale

    def gain(shape):
        return 1.0 + nrm(shape, 0.02)

    n_pages = PAST_LEN // PAGE_SIZE
    n_used = DEC_BATCH * n_pages
    n_pool = n_used + max(1, n_used // 4)
    perm = jax.random.permutation(next(ks), n_pool)
    page_table = perm[:n_used].reshape(DEC_BATCH, n_pages).astype(jnp.int32)
    pool = lambda *tail: nrm((n_pool, PAGE_SIZE) + tail)
    D = D_MODEL
    return {
        'x_prompt': nrm((BATCH, SEQ, D)),
        'x_sample': nrm((DEC_BATCH, DEC_SEQ, D)),
        'cache_mla_latent': pool(MLA_KV_RANK),
        'cache_mla_krope': pool(MLA_ROPE),
        'cache_diff_k': pool(DIFF_KV_HEADS, 2, DIFF_HD),
        'cache_diff_v': pool(DIFF_KV_HEADS, DIFF_VD),
        'cache_dsa_k': pool(DSA_KV_HEADS, DSA_HD),
        'cache_dsa_v': pool(DSA_KV_HEADS, DSA_HD),
        'cache_dsa_idx_k': pool(IDX_DIM),
        'page_table': page_table,
        'c_prompt': nrm((BATCH, D)),
        'c_sample': nrm((DEC_BATCH, D)),
        'w_ada': nrm((DEPTH, D, 6 * D), 0.5 * D ** -0.5),
        'b_ada': nrm((DEPTH, 6 * D), 0.01),
        'g_mix': gain((DEPTH, D)),
        'g_ff': gain((DEPTH, D)),
        'w_ff_up': nrm((DEPTH, D, D_FF), D ** -0.5),
        'w_ff_down': nrm((DEPTH, D_FF, D), D_FF ** -0.5),
        'g_final': gain((D,)),
        'w_in_ab': nrm((D, L0_IN), D ** -0.5),
        'g_mla_q': gain((MLA_Q_RANK,)),
        'w_mla_q_up': nrm((MLA_Q_RANK, MLA_HEADS * (MLA_NOPE + MLA_ROPE)), MLA_Q_RANK ** -0.5),
        'g_mla_kv': gain((MLA_KV_RANK,)),
        'w_mla_uk': nrm((MLA_KV_RANK, MLA_HEADS, MLA_NOPE), MLA_KV_RANK ** -0.5),
        'w_mla_uv': nrm((MLA_KV_RANK, MLA_HEADS, MLA_V), MLA_KV_RANK ** -0.5),
        'diff_lambda_q1': nrm((DIFF_HD,), 0.1),
        'diff_lambda_k1': nrm((DIFF_HD,), 0.1),
        'diff_lambda_q2': nrm((DIFF_HD,), 0.1),
        'diff_lambda_k2': nrm((DIFF_HD,), 0.1),
        'g_diff_sub': gain((DIFF_VD,)),
        'w_out_ab': nrm((L0_MIX, D), L0_MIX ** -0.5),
        'w_in_c': nrm((D, L1_IN), D ** -0.5),
        'w_out_c': nrm((L1_MIX, D), L1_MIX ** -0.5),
    }


def reference(x_prompt, x_sample, cache_mla_latent, cache_mla_krope, cache_diff_k, cache_diff_v,
              cache_dsa_k, cache_dsa_v, cache_dsa_idx_k, page_table, c_prompt, c_sample,
              w_ada, b_ada, g_mix, g_ff, w_ff_up, w_ff_down, g_final,
              w_in_ab, g_mla_q, w_mla_q_up, g_mla_kv, w_mla_uk, w_mla_uv,
              diff_lambda_q1, diff_lambda_k1, diff_lambda_q2, diff_lambda_k2, g_diff_sub, w_out_ab,
              w_in_c, w_out_c):
    past = page_table.shape[1] * PAGE_SIZE
    pos_p = jnp.arange(x_prompt.shape[1], dtype=jnp.float32)
    pos_s = float(past) + jnp.arange(x_sample.shape[1], dtype=jnp.float32)
    ab_weights = (w_in_ab, g_mla_q, w_mla_q_up, g_mla_kv, w_mla_uk, w_mla_uv,
                  diff_lambda_q1, diff_lambda_k1, diff_lambda_q2, diff_lambda_k2, g_diff_sub, w_out_ab)
    ab_cache = (cache_mla_latent, cache_mla_krope, cache_diff_k, cache_diff_v, page_table)
    c_cache = (cache_dsa_k, cache_dsa_v, cache_dsa_idx_k, page_table)
    xp, xs = x_prompt, x_sample
    for l in range(DEPTH):
        shared = (w_ada[l], b_ada[l], g_mix[l], g_ff[l], w_ff_up[l], w_ff_down[l])
        if l % 2 == 0:
            xp, (lat_p, kr_p, dk_p, dv_p) = conditioned_block(
                xp, c_prompt, lambda h: mixer_ab(h, pos_p, *ab_weights), *shared)
            xs, (lat_s, kr_s, dk_s, dv_s) = conditioned_block(
                xs, c_sample, lambda h: mixer_ab(h, pos_s, *ab_weights, paged=ab_cache), *shared)
        else:
            xp, (ck_p, cv_p, ci_p) = conditioned_block(
                xp, c_prompt, lambda h: mixer_c(h, pos_p, w_in_c, w_out_c), *shared)
            xs, (ck_s, cv_s, ci_s) = conditioned_block(
                xs, c_sample, lambda h: mixer_c(h, pos_s, w_in_c, w_out_c, paged=c_cache), *shared)
    y_prompt = rmsnorm(xp, g_final)
    y_sample = rmsnorm(xs, g_final)
    return (y_prompt, y_sample, lat_p, lat_s, kr_p, kr_s, dk_p, dk_s, dv_p, dv_s,
            ck_p, ck_s, cv_p, cv_s, ci_p, ci_s)
```

```python
import functools
import math

import jax
import jax.numpy as jnp
from jax import lax
from jax.experimental import pallas as pl
from jax.experimental.pallas import tpu as pltpu

F32 = jnp.float32
BF16 = jnp.bfloat16
I32 = jnp.int32

ROPE_THETA = 10000.0
NORM_EPS = 1e-6
PAGE = 128

MLA_HEADS = 8
MLA_NOPE = 64
MLA_ROPE = 32
MLA_V = 64
MLA_Q_RANK = 256
MLA_KV_RANK = 256
DIFF_HEADS = 4
DIFF_KV_HEADS = 2
DIFF_HD = 64
DIFF_VD = 128
DIFF_LAMBDA_INIT = 0.8 - 0.6 * math.exp(-0.3 * 0)
DSA_HEADS = 16
DSA_KV_HEADS = 2
DSA_GROUP = 8
DSA_HD = 64
IDX_HEADS = 8
IDX_DIM = 64
IDX_TOPK_MAX = 256
IDX_WEIGHT_SCALE = (IDX_HEADS ** -0.5) * (IDX_DIM ** -0.5)
MLA_SCALE = (MLA_NOPE + MLA_ROPE) ** -0.5
DIFF_SCALE = DIFF_HD ** -0.5
DSA_SCALE = DSA_HD ** -0.5

LANES = 128
NEG = -1e30
INT_MIN = -2 ** 31
VMEM_LIMIT = 56 * 1024 * 1024


def _dot(a, b):
    return jnp.dot(a, b, preferred_element_type=F32)


def _dot_nt(a, b):
    return lax.dot_general(a, b, (((1,), (1,)), ((), ())), preferred_element_type=F32)


def _rms(x, g):
    return x * lax.rsqrt(jnp.mean(x * x, axis=-1, keepdims=True) + NORM_EPS) * g


def _rope_chunk(x, cos, sin, half):
    lane = lax.broadcasted_iota(I32, x.shape, 1)
    first = (lane & (2 * half - 1)) < half
    swapped = jnp.where(first, pltpu.roll(x, LANES - half, 1), pltpu.roll(x, half, 1))
    return x * cos + swapped * sin


def _lane_mask(shape, lo, hi):
    lane = lax.broadcasted_iota(I32, shape, 1)
    return (lane >= lo) & (lane < hi)


def _cparams(sem):
    return pltpu.CompilerParams(dimension_semantics=sem, vmem_limit_bytes=VMEM_LIMIT)


def _const_spec(shape):
    return pl.BlockSpec(shape, lambda *_: (0,) * len(shape), pipeline_mode=pl.Buffered(1))


def _ada_kernel(c_ref, w_ref, b_ref, o_ref):
    c = c_ref[...]
    s = (c * (1.0 / (1.0 + jnp.exp(-c)))).astype(BF16)
    o_ref[0] = _dot(s, w_ref[0].astype(BF16)) + b_ref[0]


def _ada(c_all, w_ada, b_ada):
    depth, d, n = w_ada.shape
    rows = c_all.shape[0]
    tn = 1536
    return pl.pallas_call(
        _ada_kernel,
        grid=(depth, n // tn),
        in_specs=[pl.BlockSpec((rows, d), lambda l, j: (0, 0)),
                  pl.BlockSpec((1, d, tn), lambda l, j: (l, 0, j)),
                  pl.BlockSpec((1, 1, tn), lambda l, j: (l, 0, j))],
        out_specs=pl.BlockSpec((1, rows, tn), lambda l, j: (l, 0, j)),
        out_shape=jax.ShapeDtypeStruct((depth, rows, n), F32),
        compiler_params=_cparams(("arbitrary", "arbitrary")),
        name="ada",
    )(c_all, w_ada, b_ada.reshape(depth, 1, n))


class _Tok:
    def __init__(self, T, tm, tiles_per_group, mod_rows, d_model):
        self.T, self.tm, self.tpg, self.R, self.D = T, tm, tiles_per_group, mod_rows, d_model
        self.grid = (T // tm,)

    def tok(self, width, col=0):
        return pl.BlockSpec((self.tm, width), lambda i: (i, col))

    def mod(self, k):
        tpg = self.tpg
        return pl.BlockSpec((1, self.R, self.D), lambda i: (i // tpg, 0, k))

    def table(self):
        tpg = self.tpg
        rows = self.tm if self.R == 1 else 1
        return pl.BlockSpec((rows, LANES), lambda i: (i % tpg, 0))


def _inproj0_kernel(x_ref, sh_ref, sc_ref, gmix_ref, w_ref, gq_ref, wq_ref, gkv_ref,
                    c32_ref, s32_ref, c64_ref, s64_ref,
                    qn_ref, qr_ref, lat_ref, kr_ref, krt_ref, dq_ref, dk_ref, dv_ref):
    x = x_ref[...]
    h = _rms(x, gmix_ref[...]) * (1.0 + sc_ref[0]) + sh_ref[0]
    z = _dot(h.astype(BF16), w_ref[...])
    c32, s32, c64, s64 = c32_ref[...], s32_ref[...], c64_ref[...], s64_ref[...]
    qan = _rms(z[:, 0:256], gq_ref[...])
    q = _dot(qan.astype(BF16), wq_ref[...])
    qn_ref[...] = (q[:, 0:512] * MLA_SCALE).astype(BF16)
    for c in range(2):
        lo = 512 + c * LANES
        qr_ref[:, c * LANES:(c + 1) * LANES] = (
            _rope_chunk(q[:, lo:lo + LANES], c32, s32, 16) * MLA_SCALE).astype(BF16)
    lat_ref[...] = _rms(z[:, 256:512], gkv_ref[...])
    kr = _rope_chunk(z[:, 1536:1664], c32, s32, 16)
    kr_ref[...] = kr[:, 0:MLA_ROPE]
    krt_ref[...] = kr.astype(BF16)
    for c in range(4):
        lo = 512 + c * LANES
        dq_ref[:, c * LANES:(c + 1) * LANES] = (
            _rope_chunk(z[:, lo:lo + LANES], c64, s64, 32) * DIFF_SCALE).astype(BF16)
    for c in range(2):
        lo = 1024 + c * LANES
        dk_ref[:, c * LANES:(c + 1) * LANES] = _rope_chunk(z[:, lo:lo + LANES], c64, s64, 32)
    dv_ref[...] = z[:, 1280:1536]


def _inproj0(tk, x, ada, g_mix, w_in, g_q, w_q, g_kv, tabs):
    T, D = x.shape
    outs = [(512, BF16), (256, BF16), (256, F32), (MLA_ROPE, F32), (LANES, BF16),
            (512, BF16), (256, F32), (256, F32)]
    return pl.pallas_call(
        _inproj0_kernel,
        grid=tk.grid,
        in_specs=[tk.tok(D), tk.mod(0), tk.mod(1), _const_spec((1, D)), _const_spec(w_in.shape),
                  _const_spec((1, 256)), _const_spec(w_q.shape), _const_spec((1, 256)),
                  tk.table(), tk.table(), tk.table(), tk.table()],
        out_specs=[tk.tok(w) for w, _ in outs],
        out_shape=[jax.ShapeDtypeStruct((T, w), dt) for w, dt in outs],
        compiler_params=_cparams(("arbitrary",)),
        name="inproj0",
    )(x, ada, ada, g_mix.reshape(1, D), w_in, g_q.reshape(1, 256), w_q, g_kv.reshape(1, 256), *tabs)


def _inproj1_kernel(x_ref, sh_ref, sc_ref, gmix_ref, w_ref, c64_ref, s64_ref,
                    q_ref, k_ref, v_ref, iq_ref, ik_ref, ikt_ref, iw_ref):
    x = x_ref[...]
    h = _rms(x, gmix_ref[...]) * (1.0 + sc_ref[0]) + sh_ref[0]
    z = _dot(h.astype(BF16), w_ref[...])
    c64, s64 = c64_ref[...], s64_ref[...]
    for c in range(8):
        lo = c * LANES
        q_ref[:, lo:lo + LANES] = (_rope_chunk(z[:, lo:lo + LANES], c64, s64, 32) * DSA_SCALE).astype(BF16)
    k_ref[...] = _rope_chunk(z[:, 1024:1152], c64, s64, 32)
    v_ref[...] = z[:, 1152:1280]
    for c in range(4):
        lo = 1280 + c * LANES
        iq_ref[:, c * LANES:(c + 1) * LANES] = _rope_chunk(z[:, lo:lo + LANES], c64, s64, 32).astype(BF16)
    ik = _rope_chunk(z[:, 1792:1920], c64, s64, 32)
    ik_ref[...] = ik[:, 0:IDX_DIM]
    ikt_ref[...] = ik.astype(BF16)
    iw_ref[...] = z[:, 1920:2048] * IDX_WEIGHT_SCALE


def _inproj1(tk, x, ada, g_mix, w_in, tabs):
    T, D = x.shape
    outs = [(1024, BF16), (128, F32), (128, F32), (512, BF16), (IDX_DIM, F32), (LANES, BF16), (LANES, F32)]
    return pl.pallas_call(
        _inproj1_kernel,
        grid=tk.grid,
        in_specs=[tk.tok(D), tk.mod(0), tk.mod(1), _const_spec((1, D)), _const_spec(w_in.shape),
                  tk.table(), tk.table()],
        out_specs=[tk.tok(w) for w, _ in outs],
        out_shape=[jax.ShapeDtypeStruct((T, w), dt) for w, dt in outs],
        compiler_params=_cparams(("arbitrary",)),
        name="inproj1",
    )(x, ada, ada, g_mix.reshape(1, D), w_in, *tabs)


def _post_kernel(a_ref, b_ref, x_ref, gm_ref, shf_ref, scf_ref, gf_ref, woa_ref, wob_ref, gff_ref,
                 wup_ref, wdn_ref, gfin_ref, o_ref, *, final, ff_chunk):
    mix = _dot(a_ref[...], woa_ref[...]) + _dot(b_ref[...], wob_ref[...])
    x1 = x_ref[...] + gm_ref[0] * mix
    h = (_rms(x1, gff_ref[...]) * (1.0 + scf_ref[0]) + shf_ref[0]).astype(BF16)
    d_ff = wup_ref.shape[1]
    acc = jnp.zeros(x1.shape, F32)
    for c in range(d_ff // ff_chunk):
        u = _dot(h, wup_ref[:, c * ff_chunk:(c + 1) * ff_chunk])
        u = jnp.square(jnp.maximum(u, 0.0)).astype(BF16)
        acc = acc + _dot(u, wdn_ref[c * ff_chunk:(c + 1) * ff_chunk, :])
    x2 = x1 + gf_ref[0] * acc
    if final:
        x2 = _rms(x2, gfin_ref[...])
    o_ref[...] = x2


def _post(tk, a, a_col, b, b_col, x, ada, wo_a, wo_b, g_ff, w_up, w_dn, g_final, final):
    T, D = x.shape
    half = wo_a.shape[0]
    return pl.pallas_call(
        functools.partial(_post_kernel, final=final, ff_chunk=1024),
        grid=tk.grid,
        in_specs=[tk.tok(half, a_col), tk.tok(half, b_col), tk.tok(D), tk.mod(2), tk.mod(3), tk.mod(4), tk.mod(5),
                  _const_spec(wo_a.shape), _const_spec(wo_b.shape), _const_spec((1, D)),
                  _const_spec(w_up.shape), _const_spec(w_dn.shape), _const_spec((1, D))],
        out_specs=tk.tok(D),
        out_shape=jax.ShapeDtypeStruct((T, D), F32),
        compiler_params=_cparams(("arbitrary",)),
        name="post_final" if final else "post",
    )(a, b, x, ada, ada, ada, ada, wo_a, wo_b, g_ff.reshape(1, D), w_up, w_dn, g_final.reshape(1, D))


def _flash_causal(q_s, q2_s, k_s, k2_s, v_s, m_s, l_s, acc_s, qi, tq, tk):
    R = q_s.shape[0]
    m_s[...] = jnp.full(m_s.shape, -jnp.inf, F32)
    l_s[...] = jnp.zeros(l_s.shape, F32)
    acc_s[...] = jnp.zeros(acc_s.shape, F32)
    nkv = ((qi + 1) * tq + tk - 1) // tk
    qpos = qi * tq + (lax.broadcasted_iota(I32, (R, tk), 0) & (tq - 1))
    col = lax.broadcasted_iota(I32, (R, tk), 1)

    def body(j, carry):
        off = pl.multiple_of(j * tk, tk)
        s = _dot_nt(q_s[...], k_s[pl.ds(off, tk), :])
        if q2_s is not None:
            s = s + _dot_nt(q2_s[...], k2_s[pl.ds(off, tk), :])
        s = jnp.where(col + off <= qpos, s, NEG)
        m_prev = m_s[...]
        m_new = jnp.maximum(m_prev, jnp.max(s, axis=-1, keepdims=True))
        corr = jnp.exp(m_prev - m_new)
        p = jnp.exp(s - m_new)
        l_s[...] = l_s[...] * corr + jnp.sum(p, axis=-1, keepdims=True)
        acc_s[...] = acc_s[...] * corr + _dot(p.astype(BF16), v_s[pl.ds(off, tk), :])
        m_s[...] = m_new
        return carry

    lax.fori_loop(0, nkv, body, 0)


def _mla_out(lat_rows, wuv_ref, tq):
    chunks = []
    for c in range(MLA_HEADS // 2):
        chunks.append(_dot(lat_rows(2 * c), wuv_ref[2 * c]) + _dot(lat_rows(2 * c + 1), wuv_ref[2 * c + 1]))
    return chunks


def _mla_prompt_kernel(qn_ref, qr_ref, lat_ref, krt_ref, wuk_ref, wuv_ref, o_ref,
                       q_s, q2_s, k_s, k2_s, m_s, l_s, acc_s, *, tq, tk):
    qi = pl.program_id(1)

    @pl.when(qi == 0)
    def _():
        k_s[...] = lat_ref[0].astype(BF16)
        k2_s[...] = krt_ref[0]

    qn = qn_ref[0]
    qr = qr_ref[0]
    for c in range(MLA_HEADS // 2):
        ql = _dot(qn[:, c * LANES:(c + 1) * LANES], wuk_ref[c]).astype(BF16)
        for e in range(2):
            h = 2 * c + e
            q_s[h * tq:(h + 1) * tq, :] = ql[:, e * 256:(e + 1) * 256]
    for h in range(MLA_HEADS):
        chunk = qr[:, (h // 4) * LANES:(h // 4 + 1) * LANES]
        lo = (h % 4) * MLA_ROPE
        q2_s[h * tq:(h + 1) * tq, :] = jnp.where(_lane_mask(chunk.shape, lo, lo + MLA_ROPE), chunk,
                                                 jnp.zeros_like(chunk))
    _flash_causal(q_s, q2_s, k_s, k2_s, k_s, m_s, l_s, acc_s, qi, tq, tk)

    def lat_rows(h):
        sl = slice(h * tq, (h + 1) * tq)
        return (acc_s[sl, :] * (1.0 / l_s[sl, :])).astype(BF16)

    for c, chunk in enumerate(_mla_out(lat_rows, wuv_ref, tq)):
        o_ref[0, :, c * LANES:(c + 1) * LANES] = chunk.astype(o_ref.dtype)


def _mla_prompt(qn, qr, lat, krt, wuk_bd, wuv_pad, tq=128, tk=512):
    B, S, _ = qn.shape
    tk = min(tk, S)
    R = MLA_HEADS * tq
    return pl.pallas_call(
        functools.partial(_mla_prompt_kernel, tq=tq, tk=tk),
        grid=(B, S // tq),
        in_specs=[pl.BlockSpec((1, tq, 512), lambda b, i: (b, i, 0)),
                  pl.BlockSpec((1, tq, 256), lambda b, i: (b, i, 0)),
                  pl.BlockSpec((1, S, 256), lambda b, i: (b, 0, 0)),
                  pl.BlockSpec((1, S, LANES), lambda b, i: (b, 0, 0)),
                  _const_spec(wuk_bd.shape), _const_spec(wuv_pad.shape)],
        out_specs=pl.BlockSpec((1, tq, 512), lambda b, i: (b, i, 0)),
        out_shape=jax.ShapeDtypeStruct((B, S, 512), BF16),
        scratch_shapes=[pltpu.VMEM((R, 256), BF16), pltpu.VMEM((R, LANES), BF16),
                        pltpu.VMEM((S, 256), BF16), pltpu.VMEM((S, LANES), BF16),
                        pltpu.VMEM((R, 1), F32), pltpu.VMEM((R, 1), F32), pltpu.VMEM((R, 256), F32)],
        compiler_params=_cparams(("arbitrary", "arbitrary")),
        name="mla_prompt",
    )(qn, qr, lat, krt, wuk_bd, wuv_pad)


def _diff_lambda(lam_ref):
    lv = lam_ref[...]
    e1 = jnp.exp(jnp.sum(lv[0:1, :] * lv[1:2, :], axis=-1, keepdims=True))
    e2 = jnp.exp(jnp.sum(lv[2:3, :] * lv[3:4, :], axis=-1, keepdims=True))
    return e1 - e2 + DIFF_LAMBDA_INIT


def _diff_combine(a1, a2, lam, gsub):
    o = a1 - lam * a2
    return _rms(o, gsub) * (1.0 - DIFF_LAMBDA_INIT)


def _diff_prompt_kernel(dq_ref, dk_ref, dv_ref, lam_ref, gsub_ref, o_ref,
                        q_s, k_s, v_s, m_s, l_s, acc_s, *, tq, tk):
    qi = pl.program_id(1)

    @pl.when(qi == 0)
    def _():
        k_s[...] = dk_ref[0].astype(BF16)
        v_s[...] = dv_ref[0].astype(BF16)

    dq = dq_ref[0]
    zeros = jnp.zeros((tq, LANES), BF16)
    for j in range(2):
        for m in range(2):
            for g in range(2):
                r = (j * 2 + m) * 2 + g
                chunk = dq[:, (j * 2 + g) * LANES:(j * 2 + g + 1) * LANES]
                piece = jnp.where(_lane_mask(chunk.shape, m * DIFF_HD, (m + 1) * DIFF_HD), chunk, zeros)
                q_s[r * tq:(r + 1) * tq, j * LANES:(j + 1) * LANES] = piece
                q_s[r * tq:(r + 1) * tq, (1 - j) * LANES:(2 - j) * LANES] = zeros
    _flash_causal(q_s, None, k_s, None, v_s, m_s, l_s, acc_s, qi, tq, tk)
    lam = _diff_lambda(lam_ref)
    gsub = gsub_ref[...]
    for j in range(2):
        for g in range(2):
            r1 = (j * 2 + 0) * 2 + g
            r2 = (j * 2 + 1) * 2 + g
            a1 = acc_s[r1 * tq:(r1 + 1) * tq, j * LANES:(j + 1) * LANES] * (1.0 / l_s[r1 * tq:(r1 + 1) * tq, :])
            a2 = acc_s[r2 * tq:(r2 + 1) * tq, j * LANES:(j + 1) * LANES] * (1.0 / l_s[r2 * tq:(r2 + 1) * tq, :])
            hd = j * 2 + g
            o_ref[0, :, hd * LANES:(hd + 1) * LANES] = _diff_combine(a1, a2, lam, gsub).astype(o_ref.dtype)


def _diff_prompt(dq, dk, dv, lam_vecs, g_sub, tq=128, tk=512):
    B, S, _ = dq.shape
    tk = min(tk, S)
    R = 8 * tq
    return pl.pallas_call(
        functools.partial(_diff_prompt_kernel, tq=tq, tk=tk),
        grid=(B, S // tq),
        in_specs=[pl.BlockSpec((1, tq, 512), lambda b, i: (b, i, 0)),
                  pl.BlockSpec((1, S, 256), lambda b, i: (b, 0, 0)),
                  pl.BlockSpec((1, S, 256), lambda b, i: (b, 0, 0)),
                  _const_spec((4, DIFF_HD)), _const_spec((1, DIFF_VD))],
        out_specs=pl.BlockSpec((1, tq, 512), lambda b, i: (b, i, 0)),
        out_shape=jax.ShapeDtypeStruct((B, S, 512), BF16),
        scratch_shapes=[pltpu.VMEM((R, 256), BF16), pltpu.VMEM((S, 256), BF16), pltpu.VMEM((S, 256), BF16),
                        pltpu.VMEM((R, 1), F32), pltpu.VMEM((R, 1), F32), pltpu.VMEM((R, 256), F32)],
        compiler_params=_cparams(("arbitrary", "arbitrary")),
        name="diff_prompt",
    )(dq, dk, dv, lam_vecs, g_sub.reshape(1, DIFF_VD))


def _sort_key(x):
    bits = pltpu.bitcast(x + 0.0, I32)
    return bits ^ ((bits >> 31) & 0x7FFFFFFF)


def _topk_bias(key_s, bias_s, valid_fn, kth, idx_bits):
    R, N = key_s.shape

    def count(pred):
        return jnp.sum(jnp.where(pred, 1.0, 0.0), axis=-1, keepdims=True)

    def vbody(i, t):
        cand = t + lax.shift_left(jnp.int32(1), 31 - i)
        return jnp.where(count(key_s[...] >= cand) >= kth, cand, t)

    thr = lax.fori_loop(0, 32, vbody, jnp.full((R, 1), INT_MIN, I32))
    key = key_s[...]
    bias_s[...] = jnp.where((key >= thr) & valid_fn(), 0.0, NEG)
    need = kth - count(key > thr)
    excess = jnp.where(thr > INT_MIN, count(key == thr) - need, 0.0)

    @pl.when(jnp.max(excess) > 0)
    def _():
        def ibody(i, p):
            cand = p + lax.shift_left(jnp.int32(1), idx_bits - 1 - i)
            col = lax.broadcasted_iota(I32, (R, N), 1)
            c = count((key_s[...] == thr) & (col < cand))
            return jnp.where(c < need, cand, p)

        last = lax.fori_loop(0, idx_bits, ibody, jnp.zeros((R, 1), I32))
        k2 = key_s[...]
        col = lax.broadcasted_iota(I32, (R, N), 1)
        sel = (k2 > thr) | ((k2 == thr) & (col <= last))
        bias_s[...] = jnp.where(sel & valid_fn(), 0.0, NEG)


def _dsa_prompt_kernel(q_ref, k_ref, v_ref, iq_ref, iw_ref, ikt_ref, o_ref,
                       k_s, v_s, qbd_s, iqbd_s, ws_s, key_s, bias_s, m_s, l_s, acc_s, *, tq, tk, topk):
    qi = pl.program_id(1)
    S = k_s.shape[0]

    @pl.when(qi == 0)
    def _():
        k_s[...] = k_ref[0].astype(BF16)
        v_s[...] = v_ref[0].astype(BF16)

    iq = iq_ref[0]
    iw = iw_ref[0]
    for h in range(IDX_HEADS):
        chunk = iq[:, (h // 2) * LANES:(h // 2 + 1) * LANES]
        lo = (h % 2) * IDX_DIM
        iqbd_s[h * tq:(h + 1) * tq, :] = jnp.where(_lane_mask(chunk.shape, lo, lo + IDX_DIM), chunk,
                                                   jnp.zeros_like(chunk))
        ws_s[h * tq:(h + 1) * tq, :] = iw[:, h:h + 1]
    qpos = qi * tq + lax.broadcasted_iota(I32, (tq, tk), 0)
    for c in range(S // tk):
        r = jnp.maximum(_dot_nt(iqbd_s[...], ikt_ref[0, c * tk:(c + 1) * tk, :]), 0.0) * ws_s[...]
        score = r[0:tq]
        for h in range(1, IDX_HEADS):
            score = score + r[h * tq:(h + 1) * tq]
        causal = c * tk + lax.broadcasted_iota(I32, (tq, tk), 1) <= qpos
        key_s[:, c * tk:(c + 1) * tk] = jnp.where(causal, _sort_key(score), INT_MIN)

    def valid_fn():
        return lax.broadcasted_iota(I32, (tq, S), 1) <= qi * tq + lax.broadcasted_iota(I32, (tq, S), 0)

    _topk_bias(key_s, bias_s, valid_fn, topk, int(math.log2(S)))

    q = q_ref[0]
    for j in range(DSA_KV_HEADS):
        for g in range(DSA_GROUP):
            chunk = q[:, g * LANES:(g + 1) * LANES]
            r0 = (j * DSA_GROUP + g) * tq
            qbd_s[r0:r0 + tq, :] = jnp.where(_lane_mask(chunk.shape, j * DSA_HD, (j + 1) * DSA_HD), chunk,
                                             jnp.zeros_like(chunk))
    R = DSA_HEADS * tq
    m_s[...] = jnp.full(m_s.shape, -jnp.inf, F32)
    l_s[...] = jnp.zeros(l_s.shape, F32)
    acc_s[...] = jnp.zeros(acc_s.shape, F32)
    nkv = ((qi + 1) * tq + tk - 1) // tk

    def body(c, carry):
        off = pl.multiple_of(c * tk, tk)
        s = _dot_nt(qbd_s[...], k_s[pl.ds(off, tk), :])
        s = (s.reshape(DSA_HEADS, tq, tk) + bias_s[:, pl.ds(off, tk)][None]).reshape(R, tk)
        m_prev = m_s[...]
        m_new = jnp.maximum(m_prev, jnp.max(s, axis=-1, keepdims=True))
        corr = jnp.exp(m_prev - m_new)
        p = jnp.exp(s - m_new)
        l_s[...] = l_s[...] * corr + jnp.sum(p, axis=-1, keepdims=True)
        acc_s[...] = acc_s[...] * corr + _dot(p.astype(BF16), v_s[pl.ds(off, tk), :])
        m_s[...] = m_new
        return carry

    lax.fori_loop(0, nkv, body, 0)
    low = _lane_mask((tq, LANES), 0, DSA_HD)
    for g in range(DSA_GROUP):
        r0 = g * tq
        r1 = (DSA_GROUP + g) * tq
        o0 = acc_s[r0:r0 + tq, :] * (1.0 / l_s[r0:r0 + tq, :])
        o1 = acc_s[r1:r1 + tq, :] * (1.0 / l_s[r1:r1 + tq, :])
        o_ref[0, :, g * LANES:(g + 1) * LANES] = jnp.where(low, o0, o1).astype(o_ref.dtype)


def _dsa_prompt(q, k, v, iq, iw, ikt, tq=128, tk=512):
    B, S, _ = q.shape
    tk = min(tk, S)
    topk = min(IDX_TOPK_MAX, S // 4)
    R = DSA_HEADS * tq
    return pl.pallas_call(
        functools.partial(_dsa_prompt_kernel, tq=tq, tk=tk, topk=topk),
        grid=(B, S // tq),
        in_specs=[pl.BlockSpec((1, tq, 1024), lambda b, i: (b, i, 0)),
                  pl.BlockSpec((1, S, LANES), lambda b, i: (b, 0, 0)),
                  pl.BlockSpec((1, S, LANES), lambda b, i: (b, 0, 0)),
                  pl.BlockSpec((1, tq, 512), lambda b, i: (b, i, 0)),
                  pl.BlockSpec((1, tq, LANES), lambda b, i: (b, i, 0)),
                  pl.BlockSpec((1, S, LANES), lambda b, i: (b, 0, 0))],
        out_specs=pl.BlockSpec((1, tq, 1024), lambda b, i: (b, i, 0)),
        out_shape=jax.ShapeDtypeStruct((B, S, 1024), BF16),
        scratch_shapes=[pltpu.VMEM((S, LANES), BF16), pltpu.VMEM((S, LANES), BF16),
                        pltpu.VMEM((R, LANES), BF16), pltpu.VMEM((IDX_HEADS * tq, LANES), BF16),
                        pltpu.VMEM((IDX_HEADS * tq, 1), F32),
                        pltpu.VMEM((tq, S), I32), pltpu.VMEM((tq, S), F32),
                        pltpu.VMEM((R, 1), F32), pltpu.VMEM((R, 1), F32), pltpu.VMEM((R, LANES), F32)],
        compiler_params=_cparams(("arbitrary", "arbitrary")),
        name="dsa_prompt",
    )(q, k, v, iq, iw, ikt)


def _page_specs(pool, pages_per_step, n_pages):
    width = pool.shape[-1]
    specs = []
    for j in range(pages_per_step):
        specs.append(pl.BlockSpec(
            (1, PAGE, width),
            lambda b, c, pt, j=j: (pt[b * n_pages + c * pages_per_step + j], 0, 0)))
    return specs


def _decode_kernel(pt_ref, *refs, P, has_k2, has_v, has_bias):
    it = iter(refs)
    q_ref = next(it)
    q2_ref = next(it) if has_k2 else None
    k_refs = [next(it) for _ in range(P)]
    k2_refs = [next(it) for _ in range(P)] if has_k2 else None
    v_refs = [next(it) for _ in range(P)] if has_v else None
    bias_ref = next(it) if has_bias else None
    biasn_ref = next(it) if has_bias else None
    kn_ref = next(it)
    k2n_ref = next(it) if has_k2 else None
    vn_ref = next(it)
    o_ref = next(it)
    kc_s = next(it)
    k2c_s = next(it) if has_k2 else None
    vc_s = next(it) if has_v else kc_s
    m_s, l_s, acc_s = next(it), next(it), next(it)

    c = pl.program_id(1)

    @pl.when(c == 0)
    def _():
        m_s[...] = jnp.full(m_s.shape, -jnp.inf, F32)
        l_s[...] = jnp.zeros(l_s.shape, F32)
        acc_s[...] = jnp.zeros(acc_s.shape, F32)

    for j in range(P):
        kc_s[j * PAGE:(j + 1) * PAGE, :] = k_refs[j][0].astype(BF16)
        if has_k2:
            k2c_s[j * PAGE:(j + 1) * PAGE, :] = k2_refs[j][0].astype(BF16)
        if has_v:
            vc_s[j * PAGE:(j + 1) * PAGE, :] = v_refs[j][0].astype(BF16)
    q = q_ref[0]
    s = _dot_nt(q, kc_s[...])
    if has_k2:
        s = s + _dot_nt(q2_ref[0], k2c_s[...])
    if has_bias:
        s = s + bias_ref[0]
    m_prev = m_s[...]
    m_new = jnp.maximum(m_prev, jnp.max(s, axis=-1, keepdims=True))
    corr = jnp.exp(m_prev - m_new)
    p = jnp.exp(s - m_new)
    l_s[...] = l_s[...] * corr + jnp.sum(p, axis=-1, keepdims=True)
    acc_s[...] = acc_s[...] * corr + _dot(p.astype(BF16), vc_s[...])
    m_s[...] = m_new

    @pl.when(c == pl.num_programs(1) - 1)
    def _():
        sn = jnp.sum(q.astype(F32) * kn_ref[0], axis=-1, keepdims=True)
        if has_k2:
            sn = sn + jnp.sum(q2_ref[0].astype(F32) * k2n_ref[0], axis=-1, keepdims=True)
        if has_bias:
            sn = sn + biasn_ref[0][:, 0:1]
        m_prev = m_s[...]
        m_new = jnp.maximum(m_prev, sn)
        corr = jnp.exp(m_prev - m_new)
        pn = jnp.exp(sn - m_new)
        l = l_s[...] * corr + pn
        acc = acc_s[...] * corr + pn * vn_ref[0]
        o_ref[0] = acc * (1.0 / l)


def _paged_decode(q, q2, kpool, k2pool, vpool, pt_flat, n_pages, bias, bias_new, k_new, k2_new, v_new):
    B, H, Dk = q.shape
    Dv = v_new.shape[-1]
    P = min(16, n_pages)
    has_k2, has_v, has_bias = k2pool is not None, vpool is not None, bias is not None
    row = lambda w: pl.BlockSpec((1, 1, w), lambda b, c, pt: (b, 0, 0))
    in_specs = [pl.BlockSpec((1, H, Dk), lambda b, c, pt: (b, 0, 0))]
    args = [q]
    if has_k2:
        in_specs.append(pl.BlockSpec((1, H, q2.shape[-1]), lambda b, c, pt: (b, 0, 0)))
        args.append(q2)
    in_specs += _page_specs(kpool, P, n_pages)
    args += [kpool] * P
    if has_k2:
        in_specs += _page_specs(k2pool, P, n_pages)
        args += [k2pool] * P
    if has_v:
        in_specs += _page_specs(vpool, P, n_pages)
        args += [vpool] * P
    if has_bias:
        in_specs += [pl.BlockSpec((1, 1, P * PAGE), lambda b, c, pt: (b, 0, c)), row(LANES)]
        args += [bias, bias_new]
    in_specs.append(row(Dk))
    args.append(k_new)
    if has_k2:
        in_specs.append(row(k2_new.shape[-1]))
        args.append(k2_new)
    in_specs.append(row(Dv))
    args.append(v_new)
    scratch = [pltpu.VMEM((P * PAGE, Dk), BF16)]
    if has_k2:
        scratch.append(pltpu.VMEM((P * PAGE, k2pool.shape[-1]), BF16))
    if has_v:
        scratch.append(pltpu.VMEM((P * PAGE, Dv), BF16))
    scratch += [pltpu.VMEM((H, 1), F32), pltpu.VMEM((H, 1), F32), pltpu.VMEM((H, Dv), F32)]
    return pl.pallas_call(
        functools.partial(_decode_kernel, P=P, has_k2=has_k2, has_v=has_v, has_bias=has_bias),
        grid_spec=pltpu.PrefetchScalarGridSpec(
            num_scalar_prefetch=1, grid=(B, n_pages // P), in_specs=in_specs,
            out_specs=pl.BlockSpec((1, H, Dv), lambda b, c, pt: (b, 0, 0)),
            scratch_shapes=scratch),
        out_shape=jax.ShapeDtypeStruct((B, H, Dv), F32),
        compiler_params=_cparams(("arbitrary", "arbitrary")),
        name="paged_decode",
    )(pt_flat, *args)


def _idx_decode_kernel(pt_ref, *refs, P):
    q_ref, w_ref = refs[0], refs[1]
    k_refs = refs[2:2 + P]
    kn_ref, o_ref, on_ref, kc_s = refs[2 + P:]
    for j in range(P):
        kc_s[j * PAGE:(j + 1) * PAGE, :] = k_refs[j][0].astype(BF16)
    q = q_ref[0]
    w = w_ref[0]
    r = jnp.maximum(_dot_nt(q, kc_s[...]), 0.0) * w
    o_ref[0] = jnp.sum(r, axis=0, keepdims=True)

    @pl.when(pl.program_id(1) == pl.num_programs(1) - 1)
    def _():
        rn = jnp.maximum(jnp.sum(q.astype(F32) * kn_ref[0], axis=-1, keepdims=True), 0.0) * w
        on_ref[0] = jnp.broadcast_to(jnp.sum(rn, axis=0, keepdims=True), (1, LANES))


def _idx_decode(q, w, kpool, pt_flat, n_pages, k_new):
    B, H, Dk = q.shape
    P = min(16, n_pages)
    in_specs = [pl.BlockSpec((1, H, Dk), lambda b, c, pt: (b, 0, 0)),
                pl.BlockSpec((1, H, 1), lambda b, c, pt: (b, 0, 0))]
    in_specs += _page_specs(kpool, P, n_pages)
    in_specs.append(pl.BlockSpec((1, 1, Dk), lambda b, c, pt: (b, 0, 0)))
    return pl.pallas_call(
        functools.partial(_idx_decode_kernel, P=P),
        grid_spec=pltpu.PrefetchScalarGridSpec(
            num_scalar_prefetch=1, grid=(B, n_pages // P), in_specs=in_specs,
            out_specs=[pl.BlockSpec((1, 1, P * PAGE), lambda b, c, pt: (b, 0, c)),
                       pl.BlockSpec((1, 1, LANES), lambda b, c, pt: (b, 0, 0))],
            scratch_shapes=[pltpu.VMEM((P * PAGE, Dk), BF16)]),
        out_shape=[jax.ShapeDtypeStruct((B, 1, n_pages * PAGE), F32),
                   jax.ShapeDtypeStruct((B, 1, LANES), F32)],
        compiler_params=_cparams(("arbitrary", "arbitrary")),
        name="idx_decode",
    )(pt_flat, q, w, *([kpool] * P), k_new)


def _select_decode_kernel(sc_ref, scn_ref, bias_ref, biasn_ref, key_s, b_s, *, topk, idx_bits):
    B, past = sc_ref.shape
    key_s[:, 0:past] = _sort_key(sc_ref[...])
    key_s[:, past:past + LANES] = jnp.where(_lane_mask((B, LANES), 0, 1), _sort_key(scn_ref[...]), INT_MIN)

    def valid_fn():
        return lax.broadcasted_iota(I32, key_s.shape, 1) <= past

    _topk_bias(key_s, b_s, valid_fn, topk, idx_bits)
    bias_ref[...] = b_s[:, 0:past]
    biasn_ref[...] = b_s[:, past:past + LANES]


def _select_decode(scores, score_new):
    B, past = scores.shape
    topk = min(IDX_TOPK_MAX, (past + 1) // 4)
    n = past + LANES
    return pl.pallas_call(
        functools.partial(_select_decode_kernel, topk=topk, idx_bits=int(math.ceil(math.log2(n)))),
        out_shape=[jax.ShapeDtypeStruct((B, past), F32), jax.ShapeDtypeStruct((B, LANES), F32)],
        scratch_shapes=[pltpu.VMEM((B, n), I32), pltpu.VMEM((B, n), F32)],
        compiler_params=pltpu.CompilerParams(vmem_limit_bytes=VMEM_LIMIT),
        name="select_decode",
    )(scores, score_new)


def _qlat_kernel(qn_ref, wuk_ref, o_ref):
    qn = qn_ref[...]
    for c in range(MLA_HEADS // 2):
        o_ref[:, c * 512:(c + 1) * 512] = _dot(qn[:, c * LANES:(c + 1) * LANES], wuk_ref[c]).astype(BF16)


def _qlat(qn, wuk_bd):
    T = qn.shape[0]
    return pl.pallas_call(
        _qlat_kernel,
        out_shape=jax.ShapeDtypeStruct((T, MLA_HEADS * MLA_KV_RANK), BF16),
        name="qlat_sample",
    )(qn, wuk_bd)


def _mix0_sample_kernel(lat_ref, da_ref, wuv_ref, lam_ref, gsub_ref, mla_ref, diff_ref):
    T = lat_ref.shape[0]

    def lat_rows(h):
        return lat_ref[:, h * 256:(h + 1) * 256].astype(BF16)

    for c, chunk in enumerate(_mla_out(lat_rows, wuv_ref, T)):
        mla_ref[:, c * LANES:(c + 1) * LANES] = chunk.astype(mla_ref.dtype)
    lam = _diff_lambda(lam_ref)
    gsub = gsub_ref[...]
    for j in range(2):
        for g in range(2):
            r1 = (j * 2 + 0) * 2 + g
            r2 = (j * 2 + 1) * 2 + g
            a1 = da_ref[:, r1 * LANES:(r1 + 1) * LANES]
            a2 = da_ref[:, r2 * LANES:(r2 + 1) * LANES]
            hd = j * 2 + g
            diff_ref[:, hd * LANES:(hd + 1) * LANES] = _diff_combine(a1, a2, lam, gsub).astype(diff_ref.dtype)


def _mix0_sample(lat, da, wuv_pad, lam_vecs, g_sub):
    T = lat.shape[0]
    return pl.pallas_call(
        _mix0_sample_kernel,
        out_shape=[jax.ShapeDtypeStruct((T, 512), BF16), jax.ShapeDtypeStruct((T, 512), BF16)],
        name="mix0_sample",
    )(lat, da, wuv_pad, lam_vecs, g_sub.reshape(1, DIFF_VD))


def _rope_tables(pos, d):
    half = d // 2
    inv = jnp.power(ROPE_THETA, -jnp.arange(0, d, 2, dtype=F32) / d)
    ang = pos[:, None] * inv[None, :]
    cos, sin = jnp.cos(ang), jnp.sin(ang)
    reps = LANES // d
    return (jnp.tile(jnp.concatenate([cos, cos], -1), (1, reps)),
            jnp.tile(jnp.concatenate([-sin, sin], -1), (1, reps)))


def _prep_w_in_ab(w):
    qa, lat, kr = w[:, 0:256], w[:, 256:512], w[:, 512:544]
    rest = w[:, 544:1568]
    return jnp.concatenate([qa, lat, rest, kr, kr, kr, kr], axis=1).astype(BF16)


def _prep_w_q_up(w):
    w3 = w.reshape(MLA_Q_RANK, MLA_HEADS, MLA_NOPE + MLA_ROPE)
    return jnp.concatenate([w3[:, :, :MLA_NOPE].reshape(MLA_Q_RANK, -1),
                            w3[:, :, MLA_NOPE:].reshape(MLA_Q_RANK, -1)], axis=1).astype(BF16)


def _prep_w_uk(w_uk):
    wt = jnp.transpose(w_uk, (1, 2, 0))
    z = jnp.zeros_like(wt[0])
    blocks = []
    for c in range(MLA_HEADS // 2):
        top = jnp.concatenate([wt[2 * c], z], axis=1)
        bot = jnp.concatenate([z, wt[2 * c + 1]], axis=1)
        blocks.append(jnp.concatenate([top, bot], axis=0))
    return jnp.stack(blocks).astype(BF16)


def _prep_w_uv(w_uv):
    wt = jnp.transpose(w_uv, (1, 0, 2))
    z = jnp.zeros_like(wt[0])
    return jnp.stack([jnp.concatenate([wt[h], z] if h % 2 == 0 else [z, wt[h]], axis=1)
                      for h in range(MLA_HEADS)]).astype(BF16)


def _dsa_head_perm():
    return [j * DSA_GROUP + g for g in range(DSA_GROUP) for j in range(DSA_KV_HEADS)]


def _prep_w_in_c(w):
    D = w.shape[0]
    q = w[:, 0:1024].reshape(D, DSA_HEADS, DSA_HD)
    q = q[:, jnp.array(_dsa_head_perm()), :].reshape(D, 1024)
    k, v, iq = w[:, 1024:1152], w[:, 1152:1280], w[:, 1280:1792]
    iw, ik = w[:, 1792:1800], w[:, 1800:1864]
    pad = jnp.zeros((D, LANES - IDX_HEADS), w.dtype)
    return jnp.concatenate([q, k, v, iq, ik, ik, iw, pad], axis=1).astype(BF16)


def _prep_w_out_c(w):
    D = w.shape[1]
    w3 = w.reshape(DSA_HEADS, DSA_HD, D)
    return w3[jnp.array(_dsa_head_perm())].reshape(DSA_HEADS * DSA_HD, D).astype(BF16)


def kernel(x_prompt, x_sample, cache_mla_latent, cache_mla_krope, cache_diff_k, cache_diff_v,
           cache_dsa_k, cache_dsa_v, cache_dsa_idx_k, page_table, c_prompt, c_sample,
           w_ada, b_ada, g_mix, g_ff, w_ff_up, w_ff_down, g_final,
           w_in_ab, g_mla_q, w_mla_q_up, g_mla_kv, w_mla_uk, w_mla_uv,
           diff_lambda_q1, diff_lambda_k1, diff_lambda_q2, diff_lambda_k2, g_diff_sub, w_out_ab,
           w_in_c, w_out_c):
    B, S, D = x_prompt.shape
    Bs, Ts, _ = x_sample.shape
    assert Ts == 1, "the decode kernels handle one new token per sequence"
    n_pages = page_table.shape[1]
    past = n_pages * PAGE
    n_pool = cache_mla_latent.shape[0]
    Tp = B * S

    w_in0 = _prep_w_in_ab(w_in_ab)
    w_q = _prep_w_q_up(w_mla_q_up)
    wuk_bd = _prep_w_uk(w_mla_uk)
    wuv_pad = _prep_w_uv(w_mla_uv)
    w_in1 = _prep_w_in_c(w_in_c)
    wo_ab = w_out_ab.astype(BF16)
    wo_c = _prep_w_out_c(w_out_c)
    w_up = w_ff_up.astype(BF16)
    w_dn = w_ff_down.astype(BF16)
    lam_vecs = jnp.stack([diff_lambda_q1, diff_lambda_k1, diff_lambda_q2, diff_lambda_k2])
    pos_p = jnp.arange(S, dtype=F32)
    pos_s = float(past) + jnp.arange(Ts, dtype=F32)
    tabs_p = _rope_tables(pos_p, 32) + _rope_tables(pos_p, 64)
    tabs_s = _rope_tables(pos_s, 32) + _rope_tables(pos_s, 64)
    pt_flat = page_table.reshape(-1).astype(I32)

    rows = B + Bs
    rows_pad = -(-rows // 16) * 16
    c_all = jnp.concatenate([c_prompt, c_sample, jnp.zeros((rows_pad - rows, D), F32)], axis=0)
    ada = _ada(c_all, w_ada, b_ada)
    ada_p = [ada[l, :B].reshape(B, 1, 6 * D) for l in range(2)]
    ada_s = [ada[l, B:rows].reshape(1, Bs, 6 * D) for l in range(2)]

    tm = min(256, S)
    tk_p = _Tok(Tp, tm, S // tm, 1, D)
    tk_s = _Tok(Bs, Bs, 1, Bs, D)
    xp = x_prompt.reshape(Tp, D)
    xs = x_sample.reshape(Bs, D)

    qn_p, qr_p, lat_p, kr_p, krt_p, dq_p, dk_p, dv_p = _inproj0(
        tk_p, xp, ada_p[0], g_mix[0], w_in0, g_mla_q, w_q, g_mla_kv, tabs_p)
    qn_s, qr_s, lat_s, kr_s, _, dq_s, dk_s, dv_s = _inproj0(
        tk_s, xs, ada_s[0], g_mix[0], w_in0, g_mla_q, w_q, g_mla_kv, tabs_s)

    r3 = lambda a: a.reshape(B, S, a.shape[-1])
    mla_p = _mla_prompt(r3(qn_p), r3(qr_p), r3(lat_p), r3(krt_p), wuk_bd, wuv_pad)
    diff_p = _diff_prompt(r3(dq_p), r3(dk_p), r3(dv_p), lam_vecs, g_diff_sub)
    xp = _post(tk_p, mla_p.reshape(Tp, 512), 0, diff_p.reshape(Tp, 512), 0, xp, ada_p[0],
               wo_ab[:512], wo_ab[512:], g_ff[0], w_up[0], w_dn[0], g_final, final=False)

    HP = 16
    qlat_s = _qlat(qn_s, wuk_bd).reshape(Bs, MLA_HEADS, MLA_KV_RANK)
    pad_rows = lambda a: jnp.pad(a, ((0, 0), (0, HP - a.shape[1]), (0, 0)))
    lat_dec = _paged_decode(
        pad_rows(qlat_s), pad_rows(qr_s.reshape(Bs, MLA_HEADS, MLA_ROPE)),
        cache_mla_latent, cache_mla_krope, None, pt_flat, n_pages, None, None,
        lat_s.reshape(Bs, 1, 256), kr_s.reshape(Bs, 1, MLA_ROPE), lat_s.reshape(Bs, 1, 256))
    dq5 = dq_s.reshape(Bs, 2, 2, 2, DIFF_HD)
    dq5 = jnp.transpose(dq5, (0, 1, 3, 2, 4))
    slot = jax.nn.one_hot(jnp.arange(4), 4, dtype=dq5.dtype).reshape(2, 2, 4)
    dq_bd = jnp.einsum('bjmgd,jms->bjmgsd', dq5, slot).reshape(Bs, 8, 256)
    da_dec = _paged_decode(
        pad_rows(dq_bd), None, cache_diff_k.reshape(n_pool, PAGE, 256), None,
        cache_diff_v.reshape(n_pool, PAGE, 256), pt_flat, n_pages, None, None,
        dk_s.reshape(Bs, 1, 256), None, dv_s.reshape(Bs, 1, 256))
    da = da_dec[:, :8].reshape(Bs, 2, 2, 2, 2, DIFF_VD)
    da = jnp.stack([da[:, 0, :, :, 0], da[:, 1, :, :, 1]], axis=1).reshape(Bs, 8 * DIFF_VD)
    mla_s, diff_s = _mix0_sample(lat_dec[:, :MLA_HEADS].reshape(Bs, MLA_HEADS * 256), da,
                                 wuv_pad, lam_vecs, g_diff_sub)
    xs = _post(tk_s, mla_s, 0, diff_s, 0, xs, ada_s[0], wo_ab[:512], wo_ab[512:], g_ff[0], w_up[0], w_dn[0],
               g_final, final=False)

    tabs1_p, tabs1_s = tabs_p[2:], tabs_s[2:]
    q1_p, k1_p, v1_p, iq_p, ik_p, ikt_p, iw_p = _inproj1(tk_p, xp, ada_p[1], g_mix[1], w_in1, tabs1_p)
    q1_s, k1_s, v1_s, iq_s, ik_s, _, iw_s = _inproj1(tk_s, xs, ada_s[1], g_mix[1], w_in1, tabs1_s)

    o_p = _dsa_prompt(r3(q1_p), r3(k1_p), r3(v1_p), r3(iq_p), r3(iw_p), r3(ikt_p)).reshape(Tp, 1024)
    yp = _post(tk_p, o_p, 0, o_p, 1, xp, ada_p[1], wo_c[:512], wo_c[512:], g_ff[1], w_up[1],
               w_dn[1], g_final, final=True)

    iq16 = pad_rows(iq_s.reshape(Bs, IDX_HEADS, IDX_DIM))
    iw16 = pad_rows(iw_s[:, :IDX_HEADS].reshape(Bs, IDX_HEADS, 1))
    sc_past, sc_new = _idx_decode(iq16, iw16, cache_dsa_idx_k, pt_flat, n_pages, ik_s.reshape(Bs, 1, IDX_DIM))
    bias, bias_new = _select_decode(sc_past.reshape(Bs, past), sc_new.reshape(Bs, LANES))
    q1 = q1_s.reshape(Bs, DSA_GROUP, DSA_KV_HEADS, DSA_HD)
    q1 = jnp.transpose(q1, (0, 2, 1, 3))
    eye = jnp.eye(DSA_KV_HEADS, dtype=q1.dtype)
    q_bd = jnp.einsum('bjgd,js->bjgsd', q1, eye).reshape(Bs, DSA_HEADS, DSA_KV_HEADS * DSA_HD)
    o_dec = _paged_decode(
        q_bd, None, cache_dsa_k.reshape(n_pool, PAGE, 128), None, cache_dsa_v.reshape(n_pool, PAGE, 128),
        pt_flat, n_pages, bias.reshape(Bs, 1, past), bias_new.reshape(Bs, 1, LANES),
        k1_s.reshape(Bs, 1, 128), None, v1_s.reshape(Bs, 1, 128))
    o5 = o_dec.reshape(Bs, DSA_KV_HEADS, DSA_GROUP, DSA_KV_HEADS, DSA_HD)
    o_s = jnp.stack([o5[:, 0, :, 0], o5[:, 1, :, 1]], axis=2)
    o_s = o_s.reshape(Bs, 1024).astype(BF16)
    ys = _post(tk_s, o_s, 0, o_s, 1, xs, ada_s[1], wo_c[:512], wo_c[512:], g_ff[1], w_up[1],
               w_dn[1], g_final, final=True)

    return (yp.reshape(B, S, D), ys.reshape(Bs, Ts, D),
            lat_p.reshape(B, S, 256), lat_s.reshape(Bs, Ts, 256),
            kr_p.reshape(B, S, MLA_ROPE), kr_s.reshape(Bs, Ts, MLA_ROPE),
            dk_p.reshape(B, S, 2, 2, DIFF_HD), dk_s.reshape(Bs, Ts, 2, 2, DIFF_HD),
            dv_p.reshape(B, S, 2, DIFF_VD), dv_s.reshape(Bs, Ts, 2, DIFF_VD),
            k1_p.reshape(B, S, 2, DSA_HD), k1_s.reshape(Bs, Ts, 2, DSA_HD),
            v1_p.reshape(B, S, 2, DSA_HD), v1_s.reshape(Bs, Ts, 2, DSA_HD),
            ik_p.reshape(B, S, IDX_DIM), ik_s.reshape(Bs, Ts, IDX_DIM))
```
